```python
import math
import jax, jax.numpy as jnp
from jax import lax
import numpy as np

D_MODEL = 1024
BATCH = 16
SEQ = 4096
DEPTH = 2
DEC_BATCH = 32
DEC_SEQ = 2048
PAST_LEN = 128

GRID_W = 64
D_FF = 2816
C_A = 512
H_A = 8
N_A = 64
R_W = 64
R_A = 64
R_G = 128
LNX_EPS = 64e-5
C_B = 256
H_B = 4
N_B = 64
NA_KH = 8
NA_KW = 16
C_C = 256
H_C = 4
DQ = 32
DV = 64
Q_BLOCK = 128
RMS_EPS = 1e-6
SUBLN_EPS = 1e-5
RW_COLS = 3 * C_A + R_W + R_A + R_G
NA_COLS = 3 * C_B
DF_COLS = 3 * C_C
GATE_COLS = 3 * D_MODEL
IN_COLS = RW_COLS + NA_COLS + DF_COLS + GATE_COLS

kernel_name = 'hybrid_bidir_rwkv7_natten_diffattn_encoder'


def _rmsnorm(x, g, eps=RMS_EPS):
    x32 = x.astype(jnp.float32)
    y = x32 * lax.rsqrt(jnp.mean(x32 * x32, axis=-1, keepdims=True) + eps)
    return (y * g.astype(jnp.float32)).astype(x.dtype)


def _swiglu(x, w_gate, w_up, w_down):
    return (jax.nn.silu(x @ w_gate) * (x @ w_up)) @ w_down


def _wkv7_scan(r, decay, k, v, a, b, reverse):
    B, L, H, N = r.shape
    xs = tuple(jnp.moveaxis(t, 1, 0) for t in (r, decay, k, v, a, b))

    def step(S, inp):
        r_t, d_t, k_t, v_t, a_t, b_t = inp
        sa = jnp.einsum('bhvk,bhk->bhv', S, a_t)
        S = S * d_t[:, :, None, :] + sa[..., None] * b_t[:, :, None, :] + v_t[..., None] * k_t[:, :, None, :]
        return S, jnp.einsum('bhvk,bhk->bhv', S, r_t)

    S0 = jnp.zeros((B, H, N, N), jnp.float32)
    _, y = lax.scan(step, S0, xs, reverse=reverse)
    return jnp.moveaxis(y, 0, 1)


def _rwkv7_branch(z, mu, w0, w2, a0, a2, k_a, r_k, k_k, g2, lnx_g, lnx_b):
    B, L, _ = z.shape
    f32 = jnp.float32
    prev = jnp.pad(z, ((0, 0), (1, 0), (0, 0)))[:, :L]
    nxt = jnp.pad(z, ((0, 0), (0, 1), (0, 0)))[:, 1:]
    z = z + mu[0] * (prev - z) + mu[1] * (nxt - z)
    r, k, v, w_lo, a_lo, g_lo = jnp.split(
        z, [C_A, 2 * C_A, 3 * C_A, 3 * C_A + R_W, 3 * C_A + R_W + R_A], axis=-1)

    def heads(t):
        return t.astype(f32).reshape(B, L, H_A, N_A)

    r, k, v = heads(r), heads(k), heads(v)
    kk = k * k_k.astype(f32).reshape(H_A, N_A)
    kk = kk / jnp.maximum(jnp.sqrt(jnp.sum(kk * kk, axis=-1, keepdims=True)), 1e-12)
    wt = jnp.tanh(w_lo.astype(f32))
    a_lo = a_lo.astype(f32)

    def direction(d, reverse):
        w = -jax.nn.softplus(-(w0[d].astype(f32) + wt @ w2[d].astype(f32))) - 0.5
        decay = heads(jnp.exp(-jnp.exp(w)))
        a = heads(jax.nn.sigmoid(a0[d].astype(f32) + a_lo @ a2[d].astype(f32)))
        kd = k * (1.0 + (a - 1.0) * k_a[d].astype(f32).reshape(H_A, N_A))
        y = _wkv7_scan(r, decay, kd, v, -kk, kk * a, reverse)
        bonus = jnp.sum(r * kd * r_k[d].astype(f32), axis=-1, keepdims=True) * v
        return y, bonus

    y_f, bonus_f = direction(0, False)
    y_b, bonus_b = direction(1, True)
    y = y_f + y_b
    mean = jnp.mean(y, axis=-1, keepdims=True)
    var = jnp.mean(jnp.square(y - mean), axis=-1, keepdims=True)
    y = (y - mean) * lax.rsqrt(var + LNX_EPS)
    y = y.reshape(B, L, C_A) * lnx_g.astype(f32) + lnx_b.astype(f32) + (bonus_f + bonus_b).reshape(B, L, C_A)
    g = jax.nn.sigmoid(g_lo.astype(f32)) @ g2.astype(f32)
    return (y * g).astype(z.dtype)


def _neighbourhood_attention(q, k, v, rpb):
    B, L, H, N = q.shape
    f32 = jnp.float32
    rows = L // GRID_W
    kh = min(NA_KH, rows)
    qg = q.reshape(B, rows, GRID_W, H, N)
    kg = k.reshape(B, rows, GRID_W, H, N)
    vg = v.reshape(B, rows, GRID_W, H, N)
    cols = np.arange(GRID_W)
    col_idx = np.clip(cols - NA_KW // 2, 0, GRID_W - NA_KW)[:, None] + np.arange(NA_KW)[None, :]
    dc = col_idx - cols[:, None] + (NA_KW - 1)
    bias_c = jnp.transpose(rpb.astype(f32)[:, :, dc], (0, 2, 1, 3))
    scale = N ** -0.5

    def one_row(r):
        rs = jnp.clip(r - kh // 2, 0, rows - kh)
        k_win = lax.dynamic_slice_in_dim(kg, rs, kh, axis=1)[:, :, col_idx]
        v_win = lax.dynamic_slice_in_dim(vg, rs, kh, axis=1)[:, :, col_idx]
        q_r = lax.dynamic_index_in_dim(qg, r, axis=1, keepdims=False)
        s = jnp.einsum('bchn,bicjhn->bhcij', q_r, k_win, preferred_element_type=f32) * scale
        dr = rs + jnp.arange(kh) - r + (NA_KH - 1)
        s = s + jnp.take(bias_c, dr, axis=2)[None]
        p = jax.nn.softmax(s.reshape(B, H, GRID_W, kh * NA_KW), axis=-1).reshape(B, H, GRID_W, kh, NA_KW)
        return jnp.einsum('bhcij,bicjhn->bchn', p, v_win.astype(f32))

    out = lax.map(one_row, jnp.arange(rows))
    return jnp.transpose(out, (1, 0, 2, 3, 4)).reshape(B, L, H * N)


def _diff_attention(q, k, v, lam, lam_init, subln_g):
    B, L = q.shape[:2]
    f32 = jnp.float32
    nb = L // Q_BLOCK
    slopes = np.repeat(2.0 ** (-8.0 * np.arange(1, H_C + 1) / H_C), 2).astype(np.float32)
    pos = jnp.arange(L)
    qb = jnp.moveaxis(q.reshape(B, nb, Q_BLOCK, 2 * H_C, DQ), 1, 0)
    v32 = v.astype(f32)
    scale = DQ ** -0.5

    def one_block(args):
        q_blk, i = args
        s = jnp.einsum('bqgd,bkgd->bgqk', q_blk, k, preferred_element_type=f32) * scale
        qpos = i * Q_BLOCK + jnp.arange(Q_BLOCK)
        dist = jnp.abs(qpos[:, None] - pos[None, :]).astype(f32)
        s = s - slopes[:, None, None] * dist[None]
        p = jax.nn.softmax(s, axis=-1).reshape(B, H_C, 2, Q_BLOCK, L)
        attn = p[:, :, 0] - lam * p[:, :, 1]
        return jnp.einsum('bhqk,bkhd->bqhd', attn, v32)

    o = lax.map(one_block, (qb, jnp.arange(nb)))
    o = jnp.moveaxis(o, 0, 1).reshape(B, L, H_C, DV)
    o = o * lax.rsqrt(jnp.mean(o * o, axis=-1, keepdims=True) + SUBLN_EPS) * subln_g.astype(f32) * (1.0 - lam_init)
    return o.reshape(B, L, H_C * DV)


def _token_mixing(u, p, l):
    B, L, _ = u.shape
    f32 = jnp.float32
    proj = u @ p['w_in'][l]
    o1 = RW_COLS
    o2 = o1 + NA_COLS
    o3 = o2 + DF_COLS
    z_a, z_b, z_c, z_g = jnp.split(proj, [o1, o2, o3], axis=-1)
    y_a = _rwkv7_branch(z_a, p['rwkv_mu'][l], p['rwkv_w0'][l], p['rwkv_w2'][l], p['rwkv_a0'][l], p['rwkv_a2'][l],
                        p['rwkv_k_a'][l], p['rwkv_r_k'][l], p['rwkv_k_k'][l], p['rwkv_g2'][l],
                        p['rwkv_lnx_g'][l], p['rwkv_lnx_b'][l])
    qn, kn, vn = (t.reshape(B, L, H_B, N_B) for t in jnp.split(z_b, 3, axis=-1))
    y_b = _neighbourhood_attention(qn, kn, vn, p['na_rpb'][l]).astype(u.dtype)
    qd, kd, vd = jnp.split(z_c, 3, axis=-1)
    lam_init = 0.8 - 0.6 * math.exp(-0.3 * l)
    lp = p['diff_lam'][l].astype(f32)
    lam = jnp.exp(jnp.sum(lp[0] * lp[1])) - jnp.exp(jnp.sum(lp[2] * lp[3])) + lam_init
    y_c = _diff_attention(qd.reshape(B, L, 2 * H_C, DQ), kd.reshape(B, L, 2 * H_C, DQ),
                          vd.reshape(B, L, H_C, DV), lam, lam_init, p['diff_subln_g'][l]).astype(u.dtype)
    g_a, g_b, g_c = jnp.split(jax.nn.sigmoid(z_g), 3, axis=-1)
    m = g_a * (y_a @ p['p_a'][l]) + g_b * (y_b @ p['p_b'][l]) + g_c * (y_c @ p['p_c'][l])
    return m @ p['w_out'][l]


def _trunk(x, p):
    for l in range(DEPTH):
        x = x + 0.5 * _swiglu(_rmsnorm(x, p['ln_ffn1_g'][l]), p['ffn1_w_gate'][l], p['ffn1_w_up'][l], p['ffn1_w_down'][l])
        x = x + _token_mixing(_rmsnorm(x, p['ln_mix_g'][l]), p, l)
        x = x + 0.5 * _swiglu(_rmsnorm(x, p['ln_ffn2_g'][l]), p['ffn2_w_gate'][l], p['ffn2_w_up'][l], p['ffn2_w_down'][l])
    return _rmsnorm(x, p['final_g'])


def setup_inputs(seed: int = 0) -> dict:
    key = jax.random.key(seed)
    ks = list(jax.random.split(key, 48))
    f32 = jnp.float32

    def nrm(shape, scale):
        return scale * jax.random.normal(ks.pop(), shape, f32)

    def uni(shape, lo, hi):
        return jax.random.uniform(ks.pop(), shape, f32, lo, hi)

    return {
        'x_prompt': nrm((BATCH, SEQ, D_MODEL), 1.0),
        'x_sample': nrm((DEC_BATCH, DEC_SEQ, D_MODEL), 1.0),
        'ln_ffn1_g': 1.0 + nrm((DEPTH, D_MODEL), 0.1),
        'ffn1_w_gate': nrm((DEPTH, D_MODEL, D_FF), D_MODEL ** -0.5),
        'ffn1_w_up': nrm((DEPTH, D_MODEL, D_FF), D_MODEL ** -0.5),
        'ffn1_w_down': nrm((DEPTH, D_FF, D_MODEL), D_FF ** -0.5),
        'ln_mix_g': 1.0 + nrm((DEPTH, D_MODEL), 0.1),
        'w_in': nrm((DEPTH, D_MODEL, IN_COLS), D_MODEL ** -0.5),
        'rwkv_mu': uni((DEPTH, 2, RW_COLS), 0.0, 0.5),
        'rwkv_w0': uni((DEPTH, 2, C_A), -6.0, -1.0),
        'rwkv_w2': nrm((DEPTH, 2, R_W, C_A), 0.5 * R_W ** -0.5),
        'rwkv_a0': nrm((DEPTH, 2, C_A), 0.1),
        'rwkv_a2': nrm((DEPTH, 2, R_A, C_A), R_A ** -0.5),
        'rwkv_k_a': 1.0 + nrm((DEPTH, 2, C_A), 0.1),
        'rwkv_r_k': nrm((DEPTH, 2, H_A, N_A), 0.1),
        'rwkv_k_k': 0.85 + nrm((DEPTH, C_A), 0.05),
        'rwkv_g2': nrm((DEPTH, R_G, C_A), R_G ** -0.5),
        'rwkv_lnx_g': 1.0 + nrm((DEPTH, C_A), 0.1),
        'rwkv_lnx_b': nrm((DEPTH, C_A), 0.1),
        'na_rpb': nrm((DEPTH, H_B, 2 * NA_KH - 1, 2 * NA_KW - 1), 0.2),
        'diff_lam': nrm((DEPTH, 4, DQ), 0.1),
        'diff_subln_g': 1.0 + nrm((DEPTH, DV), 0.1),
        'p_a': nrm((DEPTH, C_A, D_MODEL), C_A ** -0.5),
        'p_b': nrm((DEPTH, C_B, D_MODEL), C_B ** -0.5),
        'p_c': nrm((DEPTH, C_C, D_MODEL), C_C ** -0.5),
        'w_out': nrm((DEPTH, D_MODEL, D_MODEL), D_MODEL ** -0.5),
        'ln_ffn2_g': 1.0 + nrm((DEPTH, D_MODEL), 0.1),
        'ffn2_w_gate': nrm((DEPTH, D_MODEL, D_FF), D_MODEL ** -0.5),
        'ffn2_w_up': nrm((DEPTH, D_MODEL, D_FF), D_MODEL ** -0.5),
        'ffn2_w_down': nrm((DEPTH, D_FF, D_MODEL), D_FF ** -0.5),
        'final_g': 1.0 + nrm((D_MODEL,), 0.1),
    }


def reference(x_prompt, x_sample, ln_ffn1_g, ffn1_w_gate, ffn1_w_up, ffn1_w_down, ln_mix_g, w_in,
              rwkv_mu, rwkv_w0, rwkv_w2, rwkv_a0, rwkv_a2, rwkv_k_a, rwkv_r_k, rwkv_k_k, rwkv_g2,
              rwkv_lnx_g, rwkv_lnx_b, na_rpb, diff_lam, diff_subln_g, p_a, p_b, p_c, w_out,
              ln_ffn2_g, ffn2_w_gate, ffn2_w_up, ffn2_w_down, final_g):
    p = dict(ln_ffn1_g=ln_ffn1_g, ffn1_w_gate=ffn1_w_gate, ffn1_w_up=ffn1_w_up, ffn1_w_down=ffn1_w_down,
             ln_mix_g=ln_mix_g, w_in=w_in, rwkv_mu=rwkv_mu, rwkv_w0=rwkv_w0, rwkv_w2=rwkv_w2,
             rwkv_a0=rwkv_a0, rwkv_a2=rwkv_a2, rwkv_k_a=rwkv_k_a, rwkv_r_k=rwkv_r_k, rwkv_k_k=rwkv_k_k,
             rwkv_g2=rwkv_g2, rwkv_lnx_g=rwkv_lnx_g, rwkv_lnx_b=rwkv_lnx_b, na_rpb=na_rpb,
             diff_lam=diff_lam, diff_subln_g=diff_subln_g, p_a=p_a, p_b=p_b, p_c=p_c, w_out=w_out,
             ln_ffn2_g=ln_ffn2_g, ffn2_w_gate=ffn2_w_gate, ffn2_w_up=ffn2_w_up, ffn2_w_down=ffn2_w_down,
             final_g=final_g)
    y_prompt = _trunk(x_prompt, p)
    y_sample = _trunk(x_sample, p)
    return (y_prompt, y_sample)
```

```python
import functools
import math

import numpy as np
import jax
import jax.numpy as jnp
from jax import lax
from jax.experimental import pallas as pl
from jax.experimental.pallas import tpu as pltpu

F32 = jnp.float32
BF16 = jnp.bfloat16

D_MODEL = 1024
D_FF = 2816
C_A, H_A, N_A = 512, 8, 64
R_W, R_A, R_G = 64, 64, 128
LNX_EPS = 64e-5
C_B, H_B, N_B = 256, 4, 64
NA_KH, NA_KW, GRID_W = 8, 16, 64
C_C, H_C, DQ, DV = 256, 4, 32, 64
RMS_EPS = 1e-6
SUBLN_EPS = 1e-5
RW_COLS = 3 * C_A + R_W + R_A + R_G
NA_COLS = 3 * C_B
DF_COLS = 3 * C_C
GATE_COLS = 3 * D_MODEL

LANES = 128
VMEM_LIMIT_BYTES = 56 * 1024 * 1024

CHUNK = 64
PAIR = 2 * N_A
N_PAIRS = C_A // PAIR
NEG_BIG = -1e30


def _cparams(*semantics):
    return pltpu.CompilerParams(dimension_semantics=semantics, vmem_limit_bytes=VMEM_LIMIT_BYTES)


def _const_spec(shape):
    nd = len(shape)
    return pl.BlockSpec(shape, lambda *_: (0,) * nd, pipeline_mode=pl.Buffered(1))


def _dot(a, b):
    return jnp.dot(a, b, preferred_element_type=F32)


def _dot_nt(a, b):
    return lax.dot_general(a, b, (((1,), (1,)), ((), ())), preferred_element_type=F32)


def _split_dot(x, w):
    hi = x.astype(BF16)
    lo = (x - hi.astype(F32)).astype(BF16)
    return _dot(hi, w) + _dot(lo, w)


def _rms(x, g):
    ms = jnp.mean(x * x, axis=-1, keepdims=True)
    return x * lax.rsqrt(ms + RMS_EPS) * g


def _ffn_body(x_ref, g_ref, wg_ref, wu_ref, wd_ref, *rest, final):
    o_ref = rest[-1]
    x = x_ref[...]
    u = _rms(x, g_ref[...]).astype(BF16)
    hg = _dot(u, wg_ref[...])
    hu = _dot(u, wu_ref[...])
    h = (hg * jax.nn.sigmoid(hg) * hu).astype(BF16)
    out = x + 0.5 * _dot(h, wd_ref[...])
    if final:
        out = _rms(out, rest[0][...])
    o_ref[...] = out


def _ffn(x2, g, wg, wu, wd, final_g=None):
    t, d = x2.shape
    tm = min(512, t)
    final = final_g is not None
    ins = [x2, g.reshape(1, d), wg, wu, wd]
    specs = [pl.BlockSpec((tm, d), lambda i: (i, 0)), _const_spec((1, d)),
             _const_spec(wg.shape), _const_spec(wu.shape), _const_spec(wd.shape)]
    if final:
        ins.append(final_g.reshape(1, d))
        specs.append(_const_spec((1, d)))
    return pl.pallas_call(
        functools.partial(_ffn_body, final=final),
        grid=(t // tm,),
        in_specs=specs,
        out_specs=pl.BlockSpec((tm, d), lambda i: (i, 0)),
        out_shape=jax.ShapeDtypeStruct((t, d), F32),
        compiler_params=_cparams("parallel"),
        name="ffn",
    )(*ins)


def _inproj_body(x_ref, xh_ref, g_ref, wa_ref, wb_ref, wc_ref, wg_ref, mu_ref,
                 za_ref, qb_ref, kb_ref, vb_ref, qc_ref, kc_ref, vc_ref, gate_ref):
    g = g_ref[...]
    u = _rms(x_ref[0], g).astype(BF16)
    uh = _rms(xh_ref[0, 0], g).astype(BF16)
    z = _dot(u, wa_ref[...])
    zh = _dot(uh, wa_ref[...])
    tm = z.shape[0]
    row = lax.broadcasted_iota(jnp.int32, z.shape, 0)
    prev = jnp.where(row == 0, zh[0:1], pltpu.roll(z, 1, 0))
    nxt = jnp.where(row == tm - 1, zh[1:2], pltpu.roll(z, tm - 1, 0))
    mu = mu_ref[...]
    za_ref[0] = z + mu[0:1] * (prev - z) + mu[1:2] * (nxt - z)
    zb = _dot(u, wb_ref[...]).astype(BF16)
    qb_ref[0] = zb[:, 0:C_B]
    kb_ref[0] = zb[:, C_B:2 * C_B]
    vb_ref[0] = zb[:, 2 * C_B:3 * C_B]
    zc = _dot(u, wc_ref[...]).astype(BF16)
    qc_ref[0] = zc[:, 0:C_C]
    kc_ref[0] = zc[:, C_C:2 * C_C]
    vc_ref[0] = zc[:, 2 * C_C:3 * C_C]
    gate_ref[0] = jax.nn.sigmoid(_dot(u, wg_ref[...])).astype(BF16)


def _inproj(x, g, w_in, mu):
    b, l, d = x.shape
    tm = min(256, l)
    nt = l // tm
    o1, o2, o3 = RW_COLS, RW_COLS + NA_COLS, RW_COLS + NA_COLS + DF_COLS
    wa, wb, wc, wg = (w_in[:, :o1].astype(BF16), w_in[:, o1:o2].astype(BF16),
                      w_in[:, o2:o3].astype(BF16), w_in[:, o3:].astype(BF16))
    zrow = jnp.zeros((b, 1, d), F32)
    before = jnp.concatenate([zrow, x[:, tm - 1:l - 1:tm]], axis=1)
    after = jnp.concatenate([x[:, tm::tm], zrow], axis=1)
    halo = jnp.concatenate([before[:, :, None], after[:, :, None], jnp.zeros((b, nt, 6, d), F32)], axis=2)

    def tile(c):
        return pl.BlockSpec((1, tm, c), lambda i, j: (i, j, 0))

    def shp(c, dt):
        return jax.ShapeDtypeStruct((b, l, c), dt)

    return pl.pallas_call(
        _inproj_body,
        grid=(b, nt),
        in_specs=[tile(d), pl.BlockSpec((1, 1, 8, d), lambda i, j: (i, j, 0, 0)), _const_spec((1, d)),
                  _const_spec(wa.shape), _const_spec(wb.shape), _const_spec(wc.shape), _const_spec(wg.shape),
                  _const_spec(mu.shape)],
        out_specs=[tile(RW_COLS)] + [tile(C_B)] * 3 + [tile(C_C)] * 3 + [tile(GATE_COLS)],
        out_shape=[shp(RW_COLS, F32)] + [shp(C_B, BF16)] * 3 + [shp(C_C, BF16)] * 3 + [shp(GATE_COLS, BF16)],
        compiler_params=_cparams("parallel", "parallel"),
        name="inproj",
    )(x, halo, g.reshape(1, d), wa, wb, wc, wg, mu)


def _block_diag(y):
    left = lax.broadcasted_iota(jnp.int32, y.shape, 1) < N_A
    zero = jnp.zeros_like(y)
    return jnp.concatenate([jnp.where(left, y, zero), jnp.where(left, zero, y)], axis=0)


def _pair_mm(x, y):
    return _dot(x.astype(BF16), _block_diag(y.astype(BF16)))


def _chunk_transforms(rt, at, kt, bt, khat, bhat, v, decay_end, reverse):
    shape = (CHUNK, PAIR)
    t_idx = lax.broadcasted_iota(jnp.int32, shape, 0)
    lane = lax.broadcasted_iota(jnp.int32, shape, 1)
    s_idx = lane & (N_A - 1)
    left = lane < N_A
    if reverse:
        strict, incl = s_idx > t_idx, s_idx >= t_idx
    else:
        strict, incl = s_idx < t_idx, s_idx <= t_idx
    same16 = (t_idx >> 4) == (s_idx >> 4)
    same32 = (t_idx >> 5) == (s_idx >> 5)
    eye = (s_idx == t_idx).astype(F32)

    lhs = jnp.concatenate([at, rt], axis=0).astype(BF16)
    btb, ktb = bt.astype(BF16), kt.astype(BF16)
    zb = jnp.zeros_like(btb)
    rhs_t = jnp.concatenate([jnp.where(left, btb, zb), jnp.where(left, zb, btb),
                             jnp.where(left, ktb, zb), jnp.where(left, zb, ktb)], axis=0)
    a_all = _dot_nt(lhs, rhs_t)
    a_ab = jnp.where(strict, a_all[0:CHUNK, 0:PAIR], 0.0)
    a_ak = jnp.where(strict, a_all[0:CHUNK, PAIR:], 0.0)
    a_rb = jnp.where(incl, a_all[CHUNK:, 0:PAIR], 0.0)
    a_rk = jnp.where(incl, a_all[CHUNK:, PAIR:], 0.0)

    l_d = jnp.where(same16, a_ab, 0.0)
    l_1 = jnp.where(jnp.logical_and(same32, jnp.logical_not(same16)), a_ab, 0.0)
    l_2 = jnp.where(same32, 0.0, a_ab)
    p = eye + l_d
    sq = _pair_mm(l_d, l_d)
    p = p + _pair_mm(p, sq)
    sq = _pair_mm(sq, sq)
    p = p + _pair_mm(p, sq)
    sq = _pair_mm(sq, sq)
    t_d = p + _pair_mm(p, sq)
    t_32 = t_d + _pair_mm(_pair_mm(t_d, l_1), t_d)
    t_inv = t_32 + _pair_mm(_pair_mm(t_32, l_2), t_32)

    w0 = _pair_mm(a_ak, v)
    au = _dot(t_inv.astype(BF16),
              jnp.concatenate([_block_diag(at.astype(BF16)), _block_diag(w0.astype(BF16))], axis=1))
    abar, u0 = au[:, 0:PAIR], au[:, PAIR:]
    ru = _dot(a_rb.astype(BF16),
              jnp.concatenate([_block_diag(abar.astype(BF16)), _block_diag(u0.astype(BF16))], axis=1))
    rbar = rt + ru[:, 0:PAIR]
    y0 = ru[:, PAIR:] + _pair_mm(a_rk, v)

    lhs_t = jnp.concatenate([bhat, khat], axis=0).T.astype(BF16)
    rhs = jnp.concatenate([jnp.concatenate([abar, u0], axis=1),
                           jnp.concatenate([jnp.zeros_like(v), v], axis=1)], axis=0).astype(BF16)
    mn = _dot(lhs_t, rhs)
    m_p = jnp.where(left, mn[0:N_A, 0:PAIR], mn[N_A:, 0:PAIR]) + eye * decay_end
    n0 = jnp.where(left, mn[0:N_A, PAIR:], mn[N_A:, PAIR:])
    return rbar, y0, m_p, n0


def _rwkv_intra_body(z_ref, w2a_ref, vec_ref, ones_ref,
                     rbf_ref, y0f_ref, mf_ref, n0f_ref, rbb_ref, y0b_ref, mb_ref, n0b_ref, bonus_ref,
                     *, chunks):
    vec = vec_ref[...]
    ones = ones_ref[...]
    kk_scale = vec[8:9]
    row = lax.broadcasted_iota(jnp.int32, (CHUNK, CHUNK), 0)
    col = lax.broadcasted_iota(jnp.int32, (CHUNK, CHUNK), 1)
    tri = [(col <= row).astype(BF16), (col >= row).astype(BF16)]
    lane_lo = lax.broadcasted_iota(jnp.int32, (CHUNK, LANES), 1)
    outs = [(rbf_ref, y0f_ref, mf_ref, n0f_ref), (rbb_ref, y0b_ref, mb_ref, n0b_ref)]

    for c in range(chunks):
        rows = slice(c * CHUNK, (c + 1) * CHUNK)
        r = z_ref[0, rows, 0:C_A]
        k = z_ref[0, rows, C_A:2 * C_A]
        v = z_ref[0, rows, 2 * C_A:3 * C_A]
        lo = z_ref[0, rows, 3 * C_A:3 * C_A + R_W + R_A]
        kk = k * kk_scale
        kk = kk / jnp.maximum(jnp.sqrt(_split_dot(kk * kk, ones)), 1e-12)
        lo_t = jnp.where(lane_lo < R_W, jnp.tanh(lo), lo).astype(BF16)
        wa = _dot(lo_t, w2a_ref[...])
        bonus = jnp.zeros((CHUNK, C_A), F32)
        for d in range(2):
            w_pre = vec[d:d + 1] + wa[:, 2 * d * C_A:(2 * d + 1) * C_A]
            neg = -w_pre
            softplus = jnp.maximum(neg, 0.0) + jnp.log(1.0 + jnp.exp(-jnp.abs(neg)))
            logd = -jnp.exp(-softplus - 0.5)
            asig = jax.nn.sigmoid(vec[2 + d:3 + d] + wa[:, (2 * d + 1) * C_A:(2 * d + 2) * C_A])
            kd = k * (1.0 + (asig - 1.0) * vec[4 + d:5 + d])
            b = kk * asig
            bonus = bonus + _split_dot(r * kd * vec[6 + d:7 + d], ones) * v
            cum = _cumsum(tri[d], logd)
            e_pos = jnp.exp(cum)
            e_neg = jnp.exp(-cum)
            e_exc = jnp.exp(cum - logd)
            last = 0 if d == 1 else CHUNK - 1
            decay_end = e_pos[last:last + 1]
            rt = r * e_pos
            at = -kk * e_exc
            kt = kd * e_neg
            bt = b * e_neg
            khat = kt * decay_end
            bhat = bt * decay_end
            rb_ref, y0_ref, m_ref, n0_ref = outs[d]
            for p in range(N_PAIRS):
                ln = slice(p * PAIR, (p + 1) * PAIR)
                rbar, y0, m_p, n0 = _chunk_transforms(rt[:, ln], at[:, ln], kt[:, ln], bt[:, ln],
                                                      khat[:, ln], bhat[:, ln], v[:, ln],
                                                      decay_end[:, ln], reverse=(d == 1))
                rb_ref[0, rows, ln] = rbar.astype(BF16)
                y0_ref[0, rows, ln] = y0
                m_ref[0, rows, ln] = m_p.astype(BF16)
                n0_ref[0, rows, ln] = n0
        bonus_ref[0, rows, :] = bonus


def _cumsum(tri, x):
    hi = x.astype(BF16)
    lo = (x - hi.astype(F32)).astype(BF16)
    return _dot(tri, hi) + _dot(tri, lo)


def _rwkv_intra(za, w2a, vec, ones):
    b, l, _ = za.shape
    chunks = 2 if l % (2 * CHUNK) == 0 else 1
    rows = chunks * CHUNK
    out_tile = pl.BlockSpec((1, rows, C_A), lambda i, j: (i, j, 0))
    f32s = jax.ShapeDtypeStruct((b, l, C_A), F32)
    bf16s = jax.ShapeDtypeStruct((b, l, C_A), BF16)
    return pl.pallas_call(
        functools.partial(_rwkv_intra_body, chunks=chunks),
        grid=(b, l // rows),
        in_specs=[pl.BlockSpec((1, rows, RW_COLS), lambda i, j: (i, j, 0)),
                  _const_spec(w2a.shape), _const_spec(vec.shape), _const_spec(ones.shape)],
        out_specs=[out_tile] * 9,
        out_shape=[bf16s, f32s, bf16s, f32s, bf16s, f32s, bf16s, f32s, f32s],
        compiler_params=_cparams("parallel", "parallel"),
        name="rwkv_intra",
    )(za, w2a, vec, ones)


def _rwkv_scan_body(rbf_ref, y0f_ref, mf_ref, n0f_ref, rbb_ref, y0b_ref, mb_ref, n0b_ref,
                    yf_ref, yb_ref, zf_ref, zb_ref, *, chunks):
    @pl.when(pl.program_id(1) == 0)
    def _():
        zf_ref[...] = jnp.zeros_like(zf_ref)
        zb_ref[...] = jnp.zeros_like(zb_ref)

    dirs = [(rbf_ref, y0f_ref, mf_ref, n0f_ref, yf_ref, zf_ref, range(chunks)),
            (rbb_ref, y0b_ref, mb_ref, n0b_ref, yb_ref, zb_ref, range(chunks - 1, -1, -1))]
    for rb_ref, y0_ref, m_ref, n0_ref, y_ref, z_ref, order in dirs:
        for p in range(N_PAIRS):
            ln = slice(p * PAIR, (p + 1) * PAIR)
            z = z_ref[:, ln]
            for c in order:
                rows = slice(c * CHUNK, (c + 1) * CHUNK)
                lhs = jnp.concatenate([rb_ref[0, rows, ln], m_ref[0, rows, ln]], axis=0)
                prod = _dot(lhs, _block_diag(z.astype(BF16)))
                y_ref[0, rows, ln] = prod[0:CHUNK] + y0_ref[0, rows, ln]
                z = prod[CHUNK:] + n0_ref[0, rows, ln]
            z_ref[:, ln] = z


def _rwkv_scan(rbf, y0f, mf, n0f, rbb, y0b, mb, n0b):
    b, l, _ = rbf.shape
    chunks = 4 if l % (4 * CHUNK) == 0 else 1
    rows = chunks * CHUNK
    steps = l // rows
    fwd = pl.BlockSpec((1, rows, C_A), lambda i, j: (i, j, 0))
    bwd = pl.BlockSpec((1, rows, C_A), lambda i, j: (i, steps - 1 - j, 0))
    out = jax.ShapeDtypeStruct((b, l, C_A), F32)
    return pl.pallas_call(
        functools.partial(_rwkv_scan_body, chunks=chunks),
        grid=(b, steps),
        in_specs=[fwd] * 4 + [bwd] * 4,
        out_specs=[fwd, bwd],
        out_shape=[out, out],
        scratch_shapes=[pltpu.VMEM((N_A, C_A), F32), pltpu.VMEM((N_A, C_A), F32)],
        compiler_params=_cparams("parallel", "arbitrary"),
        name="rwkv_scan",
    )(rbf, y0f, mf, n0f, rbb, y0b, mb, n0b)


def _natten_bias_table(rpb):
    c = np.arange(GRID_W)[:, None]
    x = np.arange(GRID_W)[None, :]
    start = np.clip(c - NA_KW // 2, 0, GRID_W - NA_KW)
    valid = (x >= start) & (x < start + NA_KW)
    dc = np.clip(x - c + (NA_KW - 1), 0, 2 * NA_KW - 2)
    var = np.arange(NA_KH)[:, None]
    i = np.arange(NA_KH)[None, :]
    dr = i - var + (NA_KH - 1)
    tab = rpb.astype(F32)[:, dr[:, :, None, None], dc[None, None, :, :]]
    tab = jnp.where(valid[None, None, None], tab, NEG_BIG)
    tab = jnp.transpose(tab, (1, 0, 3, 2, 4))
    return tab.reshape(NA_KH, H_B, GRID_W, NA_KH * GRID_W)


def _natten_body(q_ref, k_ref, v_ref, bias_ref, o_ref, *, rows):
    lane = lax.broadcasted_iota(jnp.int32, (GRID_W, C_B), 1)
    win = NA_KH * GRID_W
    scale = N_B ** -0.5

    def row_body(r, carry):
        rs = jnp.clip(r - NA_KH // 2, 0, rows - NA_KH)
        var = r - rs
        q = q_ref[0, pl.ds(pl.multiple_of(r * GRID_W, GRID_W), GRID_W), :]
        kw = k_ref[0, pl.ds(pl.multiple_of(rs * GRID_W, GRID_W), win), :]
        vw = v_ref[0, pl.ds(pl.multiple_of(rs * GRID_W, GRID_W), win), :]
        out = jnp.zeros((GRID_W, C_B), F32)
        for h in range(H_B):
            hm = jnp.logical_and(lane >= h * N_B, lane < (h + 1) * N_B)
            qh = jnp.where(hm, q, jnp.zeros_like(q))
            s = _dot_nt(qh, kw) * scale + bias_ref[var, h]
            m = jnp.max(s, axis=-1, keepdims=True)
            p = jnp.exp(s - m)
            l = jnp.sum(p, axis=-1, keepdims=True)
            pv = _dot(p.astype(BF16), vw)
            out = jnp.where(hm, pv / l, out)
        o_ref[0, pl.ds(pl.multiple_of(r * GRID_W, GRID_W), GRID_W), :] = out.astype(BF16)
        return carry

    lax.fori_loop(0, rows, row_body, 0)


def _natten(q, k, v, table):
    b, l, c = q.shape
    rows = l // GRID_W
    assert rows >= NA_KH
    seq = pl.BlockSpec((1, l, c), lambda i: (i, 0, 0))
    return pl.pallas_call(
        functools.partial(_natten_body, rows=rows),
        grid=(b,),
        in_specs=[seq, seq, seq, _const_spec(table.shape)],
        out_specs=seq,
        out_shape=jax.ShapeDtypeStruct((b, l, c), BF16),
        compiler_params=_cparams("parallel"),
        name="natten",
    )(q, k, v, table)


def _diff_body(q_ref, k_ref, v_ref, lam_ref, sg_ref, o_ref, *, seq, tq, tk, lam_init):
    qi = pl.program_id(1)
    q = q_ref[0]
    lane = lax.broadcasted_iota(jnp.int32, q.shape, 1)
    lp = lam_ref[...]
    lam = (jnp.exp(jnp.sum(lp[0:1] * lp[1:2], axis=-1, keepdims=True))
           - jnp.exp(jnp.sum(lp[2:3] * lp[3:4], axis=-1, keepdims=True)) + lam_init)
    rel = (lax.broadcasted_iota(jnp.int32, (tq, tk), 1) - lax.broadcasted_iota(jnp.int32, (tq, tk), 0)).astype(F32)
    scale = DQ ** -0.5
    q_start = qi * tq

    for h in range(H_C):
        slope = float(2.0 ** (-8.0 * (h + 1) / H_C))
        heads = []
        for j in range(2):
            g = 2 * h + j
            gm = jnp.logical_and(lane >= g * DQ, lane < (g + 1) * DQ)
            qg = jnp.where(gm, q, jnp.zeros_like(q))

            def kv_step(kb, carry, qg=qg, slope=slope, h=h):
                m, l, acc = carry
                k_start = pl.multiple_of(kb * tk, tk)
                kblk = k_ref[0, pl.ds(k_start, tk), :]
                s = _dot_nt(qg, kblk) * scale
                off = (k_start - q_start).astype(F32)
                s = s - slope * jnp.abs(rel + off)
                m_new = jnp.maximum(m, jnp.max(s, axis=-1, keepdims=True))
                alpha = jnp.exp(m - m_new)
                p = jnp.exp(s - m_new)
                l = alpha * l + jnp.sum(p, axis=-1, keepdims=True)
                acc = alpha * acc + _dot(p.astype(BF16), v_ref[0, h, pl.ds(k_start, tk), :])
                return m_new, l, acc

            init = (jnp.full((tq, 1), NEG_BIG, F32), jnp.zeros((tq, 1), F32), jnp.zeros((tq, DV), F32))
            _, l, acc = lax.fori_loop(0, seq // tk, kv_step, init)
            heads.append(acc / l)
        o = heads[0] - lam * heads[1]
        o = o * lax.rsqrt(jnp.mean(o * o, axis=-1, keepdims=True) + SUBLN_EPS) * sg_ref[...] * (1.0 - lam_init)
        o_ref[0, h] = o.astype(BF16)


def _diff_attention(q, k, v_heads, lam_params, subln_g, lam_init):
    b, l, c = q.shape
    tq = min(256, l)
    tk = min(512, l)
    return pl.pallas_call(
        functools.partial(_diff_body, seq=l, tq=tq, tk=tk, lam_init=lam_init),
        grid=(b, l // tq),
        in_specs=[pl.BlockSpec((1, tq, c), lambda i, j: (i, j, 0)),
                  pl.BlockSpec((1, l, c), lambda i, j: (i, 0, 0)),
                  pl.BlockSpec((1, H_C, l, DV), lambda i, j: (i, 0, 0, 0)),
                  _const_spec(lam_params.shape), _const_spec((1, DV))],
        out_specs=pl.BlockSpec((1, H_C, tq, DV), lambda i, j: (i, 0, j, 0)),
        out_shape=jax.ShapeDtypeStruct((b, H_C, l, DV), BF16),
        compiler_params=_cparams("parallel", "arbitrary"),
        name="diff_attention",
    )(q, k, v_heads, lam_params, subln_g.reshape(1, DV))


def _merge_body(x_ref, yf_ref, yb_ref, bonus_ref, glo_ref, ynb_ref, ydf_ref, gate_ref,
                g2_ref, lnx_ref, ones_ref, pa_ref, pb_ref, pc_ref, wout_ref, o_ref):
    ones = ones_ref[...]
    y = yf_ref[0] + yb_ref[0]
    mean = _split_dot(y, ones) * (1.0 / N_A)
    yc = y - mean
    var = _split_dot(yc * yc, ones) * (1.0 / N_A)
    lnx = lnx_ref[...]
    ya = yc * lax.rsqrt(var + LNX_EPS) * lnx[0:1] + lnx[1:2] + bonus_ref[0]
    gg = _dot(jax.nn.sigmoid(glo_ref[0]).astype(BF16), g2_ref[...])
    ya = (ya * gg).astype(BF16)
    gate = gate_ref[0].astype(F32)
    m = (gate[:, 0:D_MODEL] * _dot(ya, pa_ref[...])
         + gate[:, D_MODEL:2 * D_MODEL] * _dot(ynb_ref[0], pb_ref[...])
         + gate[:, 2 * D_MODEL:] * _dot(ydf_ref[0], pc_ref[...]))
    o_ref[0] = x_ref[0] + _dot(m.astype(BF16), wout_ref[...])


def _merge(x, yf, yb, bonus, za, y_nb, y_df, gate, g2, lnx, ones, pa, pb, pc, wout):
    b, l, d = x.shape
    tm = min(256, l)

    def tile(c):
        return pl.BlockSpec((1, tm, c), lambda i, j: (i, j, 0))

    glo_block = (3 * C_A + R_W + R_A) // R_G
    return pl.pallas_call(
        _merge_body,
        grid=(b, l // tm),
        in_specs=[tile(d), tile(C_A), tile(C_A), tile(C_A),
                  pl.BlockSpec((1, tm, R_G), lambda i, j: (i, j, glo_block)),
                  tile(C_B), tile(C_C), tile(GATE_COLS),
                  _const_spec(g2.shape), _const_spec(lnx.shape), _const_spec(ones.shape),
                  _const_spec(pa.shape), _const_spec(pb.shape), _const_spec(pc.shape), _const_spec(wout.shape)],
        out_specs=tile(d),
        out_shape=jax.ShapeDtypeStruct((b, l, d), F32),
        compiler_params=_cparams("parallel", "parallel"),
        name="merge",
    )(x, yf, yb, bonus, za, y_nb, y_df, gate, g2, lnx, ones, pa, pb, pc, wout)


def _head_ones():
    idx = np.arange(C_A) // N_A
    return jnp.asarray(idx[:, None] == idx[None, :], dtype=BF16)


def _pack_rwkv(p, l):
    w2, a2 = p['rwkv_w2'][l], p['rwkv_a2'][l]
    zero = jnp.zeros((R_W, C_A), F32)
    top = jnp.concatenate([w2[0], zero, w2[1], zero], axis=1)
    bot = jnp.concatenate([zero, a2[0], zero, a2[1]], axis=1)
    w2a = jnp.concatenate([top, bot], axis=0).astype(BF16)
    rk = p['rwkv_r_k'][l].reshape(2, C_A)
    vec = jnp.concatenate([p['rwkv_w0'][l], p['rwkv_a0'][l], p['rwkv_k_a'][l], rk,
                           p['rwkv_k_k'][l][None], jnp.zeros((7, C_A), F32)], axis=0).astype(F32)
    lnx = jnp.stack([p['rwkv_lnx_g'][l], p['rwkv_lnx_b'][l]]).astype(F32)
    return w2a, vec, lnx


def _token_mixing(x, p, l, ones):
    b, seq, _ = x.shape
    za, qb, kb, vb, qc, kc, vc, gate = _inproj(x, p['ln_mix_g'][l], p['w_in'][l], p['rwkv_mu'][l])
    w2a, vec, lnx = _pack_rwkv(p, l)
    intra = _rwkv_intra(za, w2a, vec, ones)
    yf, yb = _rwkv_scan(*intra[:8])
    y_nb = _natten(qb, kb, vb, _natten_bias_table(p['na_rpb'][l]))
    lam_init = 0.8 - 0.6 * math.exp(-0.3 * l)
    vc_heads = jnp.transpose(vc.reshape(b, seq, H_C, DV), (0, 2, 1, 3))
    y_df = _diff_attention(qc, kc, vc_heads, p['diff_lam'][l].astype(F32), p['diff_subln_g'][l].astype(F32), lam_init)
    y_df = jnp.transpose(y_df, (0, 2, 1, 3)).reshape(b, seq, C_C)
    return _merge(x, yf, yb, intra[8], za, y_nb, y_df, gate, p['rwkv_g2'][l].astype(BF16), lnx, ones,
                  p['p_a'][l].astype(BF16), p['p_b'][l].astype(BF16), p['p_c'][l].astype(BF16),
                  p['w_out'][l].astype(BF16))


def _trunk(x, p):
    b, seq, d = x.shape
    depth = p['w_in'].shape[0]
    ones = _head_ones()

    def ffn(x, pre, l, final_g=None):
        y = _ffn(x.reshape(b * seq, d), p['ln_' + pre + '_g'][l], p[pre + '_w_gate'][l].astype(BF16),
                 p[pre + '_w_up'][l].astype(BF16), p[pre + '_w_down'][l].astype(BF16), final_g)
        return y.reshape(b, seq, d)

    for l in range(depth):
        x = ffn(x, 'ffn1', l)
        x = _token_mixing(x, p, l, ones)
        x = ffn(x, 'ffn2', l, p['final_g'] if l == depth - 1 else None)
    return x


def kernel(x_prompt, x_sample, ln_ffn1_g, ffn1_w_gate, ffn1_w_up, ffn1_w_down, ln_mix_g, w_in, rwkv_mu, rwkv_w0, rwkv_w2, rwkv_a0, rwkv_a2, rwkv_k_a, rwkv_r_k, rwkv_k_k, rwkv_g2, rwkv_lnx_g, rwkv_lnx_b, na_rpb, diff_lam, diff_subln_g, p_a, p_b, p_c, w_out, ln_ffn2_g, ffn2_w_gate, ffn2_w_up, ffn2_w_down, final_g):
    p = dict(ln_ffn1_g=ln_ffn1_g, ffn1_w_gate=ffn1_w_gate, ffn1_w_up=ffn1_w_up, ffn1_w_down=ffn1_w_down,
             ln_mix_g=ln_mix_g, w_in=w_in, rwkv_mu=rwkv_mu, rwkv_w0=rwkv_w0, rwkv_w2=rwkv_w2,
             rwkv_a0=rwkv_a0, rwkv_a2=rwkv_a2, rwkv_k_a=rwkv_k_a, rwkv_r_k=rwkv_r_k, rwkv_k_k=rwkv_k_k,
             rwkv_g2=rwkv_g2, rwkv_lnx_g=rwkv_lnx_g, rwkv_lnx_b=rwkv_lnx_b, na_rpb=na_rpb,
             diff_lam=diff_lam, diff_subln_g=diff_subln_g, p_a=p_a, p_b=p_b, p_c=p_c, w_out=w_out,
             ln_ffn2_g=ln_ffn2_g, ffn2_w_gate=ffn2_w_gate, ffn2_w_up=ffn2_w_up, ffn2_w_down=ffn2_w_down,
             final_g=final_g)
    return (_trunk(x_prompt, p), _trunk(x_sample, p))
```

```python
import functools
import math

import numpy as np
import jax
import jax.numpy as jnp
from jax import lax
from jax.experimental import pallas as pl
from jax.experimental.pallas import tpu as pltpu

F32 = jnp.float32
BF16 = jnp.bfloat16

D_MODEL = 1024
D_FF = 2816
C_A, H_A, N_A = 512, 8, 64
R_W, R_A, R_G = 64, 64, 128
LNX_EPS = 64e-5
C_B, H_B, N_B = 256, 4, 64
NA_KH, NA_KW, GRID_W = 8, 16, 64
C_C, H_C, DQ, DV = 256, 4, 32, 64
RMS_EPS = 1e-6
SUBLN_EPS = 1e-5
RW_COLS = 3 * C_A + R_W + R_A + R_G
NA_COLS = 3 * C_B
DF_COLS = 3 * C_C
GATE_COLS = 3 * D_MODEL

LANES = 128
VMEM_LIMIT_BYTES = 56 * 1024 * 1024

CHUNK = 64
PAIR = 2 * N_A
N_PAIRS = C_A // PAIR
DV_AUG = DV + 16
NEG_BIG = -1e30
LOG2E = math.log2(math.e)


def _cparams(*semantics):
    return pltpu.CompilerParams(dimension_semantics=semantics, vmem_limit_bytes=VMEM_LIMIT_BYTES)


def _const_spec(shape):
    nd = len(shape)
    return pl.BlockSpec(shape, lambda *_: (0,) * nd, pipeline_mode=pl.Buffered(1))


def _dot(a, b):
    return jnp.dot(a, b, preferred_element_type=F32)


def _dot_nt(a, b):
    return lax.dot_general(a, b, (((1,), (1,)), ((), ())), preferred_element_type=F32)


def _split_dot(x, w):
    hi = x.astype(BF16)
    lo = (x - hi.astype(F32)).astype(BF16)
    return _dot(hi, w) + _dot(lo, w)


def _rms(x, g):
    ms = jnp.mean(x * x, axis=-1, keepdims=True)
    return x * lax.rsqrt(ms + RMS_EPS) * g


def _ffn_body(x_ref, g_ref, wg_ref, wu_ref, wd_ref, *rest, final):
    o_ref = rest[-1]
    x = x_ref[...]
    u = _rms(x, g_ref[...]).astype(BF16)
    hg = _dot(u, wg_ref[...])
    hu = _dot(u, wu_ref[...])
    h = (hg * jax.nn.sigmoid(hg) * hu).astype(BF16)
    out = x + 0.5 * _dot(h, wd_ref[...])
    if final:
        out = _rms(out, rest[0][...])
    o_ref[...] = out


def _ffn(x2, g, wg, wu, wd, final_g=None):
    t, d = x2.shape
    tm = min(512, t)
    final = final_g is not None
    ins = [x2, g.reshape(1, d), wg, wu, wd]
    specs = [pl.BlockSpec((tm, d), lambda i: (i, 0)), _const_spec((1, d)),
             _const_spec(wg.shape), _const_spec(wu.shape), _const_spec(wd.shape)]
    if final:
        ins.append(final_g.reshape(1, d))
        specs.append(_const_spec((1, d)))
    return pl.pallas_call(
        functools.partial(_ffn_body, final=final),
        grid=(t // tm,),
        in_specs=specs,
        out_specs=pl.BlockSpec((tm, d), lambda i: (i, 0)),
        out_shape=jax.ShapeDtypeStruct((t, d), F32),
        compiler_params=_cparams("parallel"),
        name="ffn",
    )(*ins)


def _inproj_body(x_ref, xh_ref, g_ref, wa_ref, wb_ref, wc_ref, wg_ref, mu_ref,
                 za_ref, qb_ref, kb_ref, vb_ref, qc_ref, kc_ref, vc_ref, gate_ref):
    g = g_ref[...]
    u = _rms(x_ref[0], g).astype(BF16)
    uh = _rms(xh_ref[0, 0], g).astype(BF16)
    z = _dot(u, wa_ref[...])
    zh = _dot(uh, wa_ref[...])
    tm = z.shape[0]
    row = lax.broadcasted_iota(jnp.int32, z.shape, 0)
    prev = jnp.where(row == 0, zh[0:1], pltpu.roll(z, 1, 0))
    nxt = jnp.where(row == tm - 1, zh[1:2], pltpu.roll(z, tm - 1, 0))
    mu = mu_ref[...]
    za_ref[0] = z + mu[0:1] * (prev - z) + mu[1:2] * (nxt - z)
    zb = _dot(u, wb_ref[...]).astype(BF16)
    qb_ref[0] = zb[:, 0:C_B]
    kb_ref[0] = zb[:, C_B:2 * C_B]
    vb_ref[0] = zb[:, 2 * C_B:3 * C_B]
    zc = _dot(u, wc_ref[...])
    qc_ref[0] = (zc[:, 0:C_C] * (DQ ** -0.5 * LOG2E)).astype(BF16)
    kc_ref[0] = zc[:, C_C:2 * C_C].astype(BF16)
    vc_ref[0] = zc[:, 2 * C_C:3 * C_C].astype(BF16)
    gate_ref[0] = jax.nn.sigmoid(_dot(u, wg_ref[...])).astype(BF16)


def _inproj(x, g, w_in, mu):
    b, l, d = x.shape
    tm = min(256, l)
    nt = l // tm
    o1, o2, o3 = RW_COLS, RW_COLS + NA_COLS, RW_COLS + NA_COLS + DF_COLS
    wa, wb, wc, wg = (w_in[:, :o1].astype(BF16), w_in[:, o1:o2].astype(BF16),
                      w_in[:, o2:o3].astype(BF16), w_in[:, o3:].astype(BF16))
    zrow = jnp.zeros((b, 1, d), F32)
    before = jnp.concatenate([zrow, x[:, tm - 1:l - 1:tm]], axis=1)
    after = jnp.concatenate([x[:, tm::tm], zrow], axis=1)
    halo = jnp.concatenate([before[:, :, None], after[:, :, None], jnp.zeros((b, nt, 6, d), F32)], axis=2)

    def tile(c):
        return pl.BlockSpec((1, tm, c), lambda i, j: (i, j, 0))

    def shp(c, dt):
        return jax.ShapeDtypeStruct((b, l, c), dt)

    return pl.pallas_call(
        _inproj_body,
        grid=(b, nt),
        in_specs=[tile(d), pl.BlockSpec((1, 1, 8, d), lambda i, j: (i, j, 0, 0)), _const_spec((1, d)),
                  _const_spec(wa.shape), _const_spec(wb.shape), _const_spec(wc.shape), _const_spec(wg.shape),
                  _const_spec(mu.shape)],
        out_specs=[tile(RW_COLS)] + [tile(C_B)] * 3 + [tile(C_C)] * 3 + [tile(GATE_COLS)],
        out_shape=[shp(RW_COLS, F32)] + [shp(C_B, BF16)] * 3 + [shp(C_C, BF16)] * 3 + [shp(GATE_COLS, BF16)],
        compiler_params=_cparams("parallel", "parallel"),
        name="inproj",
    )(x, halo, g.reshape(1, d), wa, wb, wc, wg, mu)


def _block_diag(y):
    left = lax.broadcasted_iota(jnp.int32, y.shape, 1) < N_A
    zero = jnp.zeros_like(y)
    return jnp.concatenate([jnp.where(left, y, zero), jnp.where(left, zero, y)], axis=0)


def _pair_mm(x, y):
    return _dot(x.astype(BF16), _block_diag(y.astype(BF16)))


def _pair_masks(reverse):
    shape = (CHUNK, PAIR)
    t_idx = lax.broadcasted_iota(jnp.int32, shape, 0)
    s_idx = lax.broadcasted_iota(jnp.int32, shape, 1) & (N_A - 1)
    if reverse:
        strict, incl = s_idx > t_idx, s_idx >= t_idx
    else:
        strict, incl = s_idx < t_idx, s_idx <= t_idx
    same16 = (t_idx >> 4) == (s_idx >> 4)
    same32 = (t_idx >> 5) == (s_idx >> 5)
    return dict(strict=strict, incl=incl, same16=same16, mid=jnp.logical_and(same32, jnp.logical_not(same16)),
                same32=same32, eye=(s_idx == t_idx).astype(F32))


def _chunk_transforms(chains):
    left = lax.broadcasted_iota(jnp.int32, (CHUNK, PAIR), 1) < N_A

    def each(fn, *cols):
        return [fn(*args) for args in zip(*cols)]

    def split_lr(y):
        zero = jnp.zeros_like(y)
        return [jnp.where(left, y, zero), jnp.where(left, zero, y)]

    def interactions(ch):
        lhs = jnp.concatenate([ch['at'], ch['rt']], axis=0).astype(BF16)
        rhs_t = jnp.concatenate(split_lr(ch['bt'].astype(BF16)) + split_lr(ch['kt'].astype(BF16)), axis=0)
        return _dot_nt(lhs, rhs_t)

    a_all = each(interactions, chains)
    a_ab = [jnp.where(ch['mask']['strict'], a[0:CHUNK, 0:PAIR], 0.0) for ch, a in zip(chains, a_all)]
    a_ak = [jnp.where(ch['mask']['strict'], a[0:CHUNK, PAIR:], 0.0) for ch, a in zip(chains, a_all)]
    a_rb = [jnp.where(ch['mask']['incl'], a[CHUNK:, 0:PAIR], 0.0) for ch, a in zip(chains, a_all)]
    a_rk = [jnp.where(ch['mask']['incl'], a[CHUNK:, PAIR:], 0.0) for ch, a in zip(chains, a_all)]

    l_d = [jnp.where(ch['mask']['same16'], a, 0.0) for ch, a in zip(chains, a_ab)]
    l_1 = [jnp.where(ch['mask']['mid'], a, 0.0) for ch, a in zip(chains, a_ab)]
    l_2 = [jnp.where(ch['mask']['same32'], 0.0, a) for ch, a in zip(chains, a_ab)]
    p = [ch['mask']['eye'] + l for ch, l in zip(chains, l_d)]
    sq = each(_pair_mm, l_d, l_d)
    for step in range(3):
        p = [a + b for a, b in zip(p, each(_pair_mm, p, sq))]
        if step < 2:
            sq = each(_pair_mm, sq, sq)
    for lvl in (l_1, l_2):
        p = [a + b for a, b in zip(p, each(_pair_mm, each(_pair_mm, p, lvl), p))]
    t_inv = p

    vs = [ch['v'] for ch in chains]
    w0 = each(_pair_mm, a_ak, vs)

    def two_block_diag(a, b):
        return jnp.concatenate([_block_diag(a.astype(BF16)), _block_diag(b.astype(BF16))], axis=1)

    au = [_dot(t.astype(BF16), two_block_diag(ch['at'], w)) for t, ch, w in zip(t_inv, chains, w0)]
    abar = [x[:, 0:PAIR] for x in au]
    u0 = [x[:, PAIR:] for x in au]
    ru = [_dot(a.astype(BF16), two_block_diag(x, y)) for a, x, y in zip(a_rb, abar, u0)]
    rkv = each(_pair_mm, a_rk, vs)
    rbar = [ch['rt'] + x[:, 0:PAIR] for ch, x in zip(chains, ru)]
    y0 = [x[:, PAIR:] + y for x, y in zip(ru, rkv)]

    def transition(ch, ab, u):
        lhs_t = jnp.concatenate([ch['bhat'], ch['khat']], axis=0).T.astype(BF16)
        rhs = jnp.concatenate([jnp.concatenate([ab, u], axis=1),
                               jnp.concatenate([jnp.zeros_like(u), ch['v']], axis=1)], axis=0).astype(BF16)
        return _dot(lhs_t, rhs)

    mn = each(transition, chains, abar, u0)
    m_p = [jnp.where(left, x[0:N_A, 0:PAIR], x[N_A:, 0:PAIR]) + ch['mask']['eye'] * ch['decay_end']
           for ch, x in zip(chains, mn)]
    n0 = [jnp.where(left, x[0:N_A, PAIR:], x[N_A:, PAIR:]) for x in mn]
    return list(zip(rbar, y0, m_p, n0))


def _cumsum(tri, x):
    hi = x.astype(BF16)
    lo = (x - hi.astype(F32)).astype(BF16)
    return _dot(tri, hi) + _dot(tri, lo)


def _rwkv_intra_body(z_ref, w2a_ref, vec_ref, ones_ref,
                     rbf_ref, y0f_ref, mf_ref, n0f_ref, rbb_ref, y0b_ref, mb_ref, n0b_ref, bonus_ref,
                     *, chunks):
    vec = vec_ref[...]
    ones = ones_ref[...]
    row = lax.broadcasted_iota(jnp.int32, (CHUNK, CHUNK), 0)
    col = lax.broadcasted_iota(jnp.int32, (CHUNK, CHUNK), 1)
    tri = [(col <= row).astype(BF16), (col >= row).astype(BF16)]
    masks = [_pair_masks(False), _pair_masks(True)]
    lane_lo = lax.broadcasted_iota(jnp.int32, (CHUNK, LANES), 1)
    outs = [(rbf_ref, y0f_ref, mf_ref, n0f_ref), (rbb_ref, y0b_ref, mb_ref, n0b_ref)]
    cs = range(chunks)
    rows = [slice(c * CHUNK, (c + 1) * CHUNK) for c in cs]

    r = [z_ref[0, rows[c], 0:C_A] for c in cs]
    k = [z_ref[0, rows[c], C_A:2 * C_A] for c in cs]
    v = [z_ref[0, rows[c], 2 * C_A:3 * C_A] for c in cs]
    lo = [z_ref[0, rows[c], 3 * C_A:3 * C_A + R_W + R_A] for c in cs]
    kk = [x * vec[8:9] for x in k]
    ssq = [_split_dot(x * x, ones) for x in kk]
    kk = [x / jnp.maximum(jnp.sqrt(s), 1e-12) for x, s in zip(kk, ssq)]
    wa = [_dot(jnp.where(lane_lo < R_W, jnp.tanh(x), x).astype(BF16), w2a_ref[...]) for x in lo]

    items = [(c, d) for c in cs for d in range(2)]
    logd, asig = [], []
    for c, d in items:
        neg = -(vec[d:d + 1] + wa[c][:, 2 * d * C_A:(2 * d + 1) * C_A])
        softplus = jnp.maximum(neg, 0.0) + jnp.log(1.0 + jnp.exp(-jnp.abs(neg)))
        logd.append(-jnp.exp(-softplus - 0.5))
        asig.append(jax.nn.sigmoid(vec[2 + d:3 + d] + wa[c][:, (2 * d + 1) * C_A:(2 * d + 2) * C_A]))
    kd = [k[c] * (1.0 + (a - 1.0) * vec[4 + d:5 + d]) for (c, d), a in zip(items, asig)]
    b = [kk[c] * a for (c, d), a in zip(items, asig)]
    bonus = [_split_dot(r[c] * x * vec[6 + d:7 + d], ones) * v[c] for (c, d), x in zip(items, kd)]
    cum = [_cumsum(tri[d], x) for (c, d), x in zip(items, logd)]
    for c in cs:
        bonus_ref[0, rows[c], :] = bonus[2 * c] + bonus[2 * c + 1]

    chains, where = [], []
    for i, (c, d) in enumerate(items):
        e_pos = jnp.exp(cum[i])
        e_neg = jnp.exp(-cum[i])
        e_exc = jnp.exp(cum[i] - logd[i])
        last = 0 if d == 1 else CHUNK - 1
        decay_end = e_pos[last:last + 1]
        rt = r[c] * e_pos
        at = -kk[c] * e_exc
        kt = kd[i] * e_neg
        bt = b[i] * e_neg
        khat = kt * decay_end
        bhat = bt * decay_end
        for p in range(N_PAIRS):
            ln = slice(p * PAIR, (p + 1) * PAIR)
            chains.append(dict(rt=rt[:, ln], at=at[:, ln], kt=kt[:, ln], bt=bt[:, ln], khat=khat[:, ln],
                               bhat=bhat[:, ln], v=v[c][:, ln], decay_end=decay_end[:, ln], mask=masks[d]))
            where.append((c, d, ln))

    for (c, d, ln), (rbar, y0, m_p, n0) in zip(where, _chunk_transforms(chains)):
        rb_ref, y0_ref, m_ref, n0_ref = outs[d]
        rb_ref[0, rows[c], ln] = rbar.astype(BF16)
        y0_ref[0, rows[c], ln] = y0
        m_ref[0, rows[c], ln] = m_p.astype(BF16)
        n0_ref[0, rows[c], ln] = n0


def _rwkv_intra(za, w2a, vec, ones):
    b, l, _ = za.shape
    chunks = 2 if l % (2 * CHUNK) == 0 else 1
    rows = chunks * CHUNK
    out_tile = pl.BlockSpec((1, rows, C_A), lambda i, j: (i, j, 0))
    f32s = jax.ShapeDtypeStruct((b, l, C_A), F32)
    bf16s = jax.ShapeDtypeStruct((b, l, C_A), BF16)
    return pl.pallas_call(
        functools.partial(_rwkv_intra_body, chunks=chunks),
        grid=(b, l // rows),
        in_specs=[pl.BlockSpec((1, rows, RW_COLS), lambda i, j: (i, j, 0)),
                  _const_spec(w2a.shape), _const_spec(vec.shape), _const_spec(ones.shape)],
        out_specs=[out_tile] * 9,
        out_shape=[bf16s, f32s, bf16s, f32s, bf16s, f32s, bf16s, f32s, f32s],
        compiler_params=_cparams("parallel", "parallel"),
        name="rwkv_intra",
    )(za, w2a, vec, ones)


def _rwkv_scan_body(rbf_ref, y0f_ref, mf_ref, n0f_ref, rbb_ref, y0b_ref, mb_ref, n0b_ref,
                    yf_ref, yb_ref, zf_ref, zb_ref, *, chunks):
    @pl.when(pl.program_id(1) == 0)
    def _():
        zf_ref[...] = jnp.zeros_like(zf_ref)
        zb_ref[...] = jnp.zeros_like(zb_ref)

    lanes = [slice(p * PAIR, (p + 1) * PAIR) for p in range(N_PAIRS)]
    chains = [(rbf_ref, y0f_ref, mf_ref, n0f_ref, yf_ref, zf_ref, ln, False) for ln in lanes]
    chains += [(rbb_ref, y0b_ref, mb_ref, n0b_ref, yb_ref, zb_ref, ln, True) for ln in lanes]
    z = [ch[5][:, ch[6]] for ch in chains]
    for step in range(chunks):
        prods = []
        for (rb_ref, y0_ref, m_ref, n0_ref, y_ref, z_ref, ln, rev), zc in zip(chains, z):
            c = chunks - 1 - step if rev else step
            rows = slice(c * CHUNK, (c + 1) * CHUNK)
            lhs = jnp.concatenate([rb_ref[0, rows, ln], m_ref[0, rows, ln]], axis=0)
            prods.append(_dot(lhs, _block_diag(zc.astype(BF16))))
        new_z = []
        for (rb_ref, y0_ref, m_ref, n0_ref, y_ref, z_ref, ln, rev), prod in zip(chains, prods):
            c = chunks - 1 - step if rev else step
            rows = slice(c * CHUNK, (c + 1) * CHUNK)
            y_ref[0, rows, ln] = prod[0:CHUNK] + y0_ref[0, rows, ln]
            new_z.append(prod[CHUNK:] + n0_ref[0, rows, ln])
        z = new_z
    for ch, zc in zip(chains, z):
        ch[5][:, ch[6]] = zc


def _rwkv_scan(rbf, y0f, mf, n0f, rbb, y0b, mb, n0b):
    b, l, _ = rbf.shape
    chunks = 4 if l % (4 * CHUNK) == 0 else 1
    rows = chunks * CHUNK
    steps = l // rows
    fwd = pl.BlockSpec((1, rows, C_A), lambda i, j: (i, j, 0))
    bwd = pl.BlockSpec((1, rows, C_A), lambda i, j: (i, steps - 1 - j, 0))
    out = jax.ShapeDtypeStruct((b, l, C_A), F32)
    return pl.pallas_call(
        functools.partial(_rwkv_scan_body, chunks=chunks),
        grid=(b, steps),
        in_specs=[fwd] * 4 + [bwd] * 4,
        out_specs=[fwd, bwd],
        out_shape=[out, out],
        scratch_shapes=[pltpu.VMEM((N_A, C_A), F32), pltpu.VMEM((N_A, C_A), F32)],
        compiler_params=_cparams("parallel", "arbitrary"),
        name="rwkv_scan",
    )(rbf, y0f, mf, n0f, rbb, y0b, mb, n0b)


def _natten_bias_table(rpb):
    c = np.arange(GRID_W)[:, None]
    x = np.arange(GRID_W)[None, :]
    start = np.clip(c - NA_KW // 2, 0, GRID_W - NA_KW)
    valid = (x >= start) & (x < start + NA_KW)
    dc = x - c + (NA_KW - 1)
    onehot = ((dc[..., None] == np.arange(2 * NA_KW - 1)) & valid[..., None]).astype(np.float32)
    cols = jnp.einsum('hrd,cxd->hrcx', rpb.astype(F32), jnp.asarray(onehot), precision=lax.Precision.HIGHEST)
    cols = jnp.where(jnp.asarray(valid)[None, None], cols, NEG_BIG)
    tab = jnp.stack([cols[:, NA_KH - 1 - var:2 * NA_KH - 1 - var] for var in range(NA_KH)], axis=0)
    tab = jnp.transpose(tab, (0, 1, 3, 2, 4))
    return tab.reshape(NA_KH, H_B, GRID_W, NA_KH * GRID_W)


def _natten_body(q_ref, k_ref, v_ref, bias_ref, o_ref, *, rows, rows_per_step):
    lane = lax.broadcasted_iota(jnp.int32, (GRID_W, C_B), 1)
    head_mask = [jnp.logical_and(lane >= h * N_B, lane < (h + 1) * N_B) for h in range(H_B)]
    win = NA_KH * GRID_W
    scale = N_B ** -0.5

    def step(i, carry):
        rs_list, items = [], []
        for j in range(rows_per_step):
            r = i * rows_per_step + j
            rs = jnp.clip(r - NA_KH // 2, 0, rows - NA_KH)
            q = q_ref[0, pl.ds(pl.multiple_of(r * GRID_W, GRID_W), GRID_W), :]
            kw = k_ref[0, pl.ds(pl.multiple_of(rs * GRID_W, GRID_W), win), :]
            vw = v_ref[0, pl.ds(pl.multiple_of(rs * GRID_W, GRID_W), win), :]
            for h in range(H_B):
                items.append((j, h, r, r - rs, q, kw, vw))
        s = [_dot_nt(jnp.where(head_mask[h], q, jnp.zeros_like(q)), kw) * scale + bias_ref[var, h]
             for (j, h, r, var, q, kw, vw) in items]
        m = [jnp.max(x, axis=-1, keepdims=True) for x in s]
        p = [jnp.exp(x - mm) for x, mm in zip(s, m)]
        l = [jnp.sum(x, axis=-1, keepdims=True) for x in p]
        pv = [_dot(x.astype(BF16), it[6]) for x, it in zip(p, items)]
        for j in range(rows_per_step):
            out = jnp.zeros((GRID_W, C_B), F32)
            for h in range(H_B):
                idx = j * H_B + h
                out = jnp.where(head_mask[h], pv[idx] / l[idx], out)
            r = items[j * H_B][2]
            o_ref[0, pl.ds(pl.multiple_of(r * GRID_W, GRID_W), GRID_W), :] = out.astype(BF16)
        return carry

    lax.fori_loop(0, rows // rows_per_step, step, 0)


def _natten(q, k, v, table):
    b, l, c = q.shape
    rows = l // GRID_W
    assert rows >= NA_KH
    rows_per_step = 2 if rows % 2 == 0 else 1
    seq = pl.BlockSpec((1, l, c), lambda i: (i, 0, 0))
    return pl.pallas_call(
        functools.partial(_natten_body, rows=rows, rows_per_step=rows_per_step),
        grid=(b,),
        in_specs=[seq, seq, seq, _const_spec(table.shape)],
        out_specs=seq,
        out_shape=jax.ShapeDtypeStruct((b, l, c), BF16),
        compiler_params=_cparams("parallel"),
        name="natten",
    )(q, k, v, table)


def _diff_body(q_ref, k_ref, vt_ref, lam_ref, sg_ref, o_ref, *, seq, tq, tk, lam_init):
    qi = pl.program_id(1)
    q = q_ref[0]
    lane = lax.broadcasted_iota(jnp.int32, q.shape, 1)
    lp = lam_ref[...]
    lam = (jnp.exp(jnp.sum(lp[0:1] * lp[1:2], axis=-1, keepdims=True))
           - jnp.exp(jnp.sum(lp[2:3] * lp[3:4], axis=-1, keepdims=True)) + lam_init)
    rel = (lax.broadcasted_iota(jnp.int32, (tk, tq), 0) - lax.broadcasted_iota(jnp.int32, (tk, tq), 1)).astype(F32)
    q_start = qi * tq
    k_diag = qi // (tk // tq)
    n_k = seq // tk

    groups = range(2 * H_C)
    slope2 = [float(2.0 ** (-8.0 * (g // 2 + 1) / H_C)) * LOG2E for g in groups]
    rel_s = [slope2[2 * h] * rel for h in range(H_C)]
    qg = [jnp.where(jnp.logical_and(lane >= g * DQ, lane < (g + 1) * DQ), q, jnp.zeros_like(q)) for g in groups]

    def kv_step(kb, carry, mode):
        k_start = pl.multiple_of(kb * tk, tk)
        kblk = k_ref[0, pl.ds(k_start, tk), :]
        vblk = [vt_ref[0, h, :, pl.ds(k_start, tk)] for h in range(H_C)]
        off = (k_start - q_start).astype(F32)
        s = [_dot_nt(kblk, x) for x in qg]
        if mode == "before":
            u = [x + rel_s[g // 2] for g, x in zip(groups, s)]
            c = [-slope2[g] * off for g in groups]
        elif mode == "after":
            u = [x - rel_s[g // 2] for g, x in zip(groups, s)]
            c = [slope2[g] * off for g in groups]
        else:
            u = [x - jnp.abs(rel_s[g // 2] + slope2[g] * off) for g, x in zip(groups, s)]
            c = [0.0 for g in groups]
        m_new = [jnp.maximum(st[0], jnp.max(x, axis=0, keepdims=True) - cc) for st, x, cc in zip(carry, u, c)]
        alpha = [jnp.exp2(st[0] - mn) for st, mn in zip(carry, m_new)]
        p = [jnp.exp2(x - (mn + cc)).astype(BF16) for x, mn, cc in zip(u, m_new, c)]
        pv = [_dot(vblk[g // 2], x) for g, x in zip(groups, p)]
        acc = [a * st[1] + x for a, st, x in zip(alpha, carry, pv)]
        return tuple((mn, ac) for mn, ac in zip(m_new, acc))

    init = tuple((jnp.full((1, tq), NEG_BIG, F32), jnp.zeros((DV_AUG, tq), F32)) for _ in groups)
    carry = lax.fori_loop(0, k_diag, functools.partial(kv_step, mode="before"), init)
    carry = kv_step(k_diag, carry, "diag")
    carry = lax.fori_loop(k_diag + 1, n_k, functools.partial(kv_step, mode="after"), carry)
    for h in range(H_C):
        a0, a1 = carry[2 * h][1], carry[2 * h + 1][1]
        o = a0[0:DV] / a0[DV:DV + 1] - lam * (a1[0:DV] / a1[DV:DV + 1])
        o = o * lax.rsqrt(jnp.mean(o * o, axis=0, keepdims=True) + SUBLN_EPS) * sg_ref[...] * (1.0 - lam_init)
        o_ref[0, h] = o.astype(BF16)


def _diff_attention(q, k, v_t, lam_params, subln_g, lam_init):
    b, l, c = q.shape
    tq = min(256, l)
    tk = min(512, l)
    v_t = jnp.concatenate([v_t, jnp.ones((b, H_C, 1, l), v_t.dtype),
                           jnp.zeros((b, H_C, DV_AUG - DV - 1, l), v_t.dtype)], axis=2)
    return pl.pallas_call(
        functools.partial(_diff_body, seq=l, tq=tq, tk=tk, lam_init=lam_init),
        grid=(b, l // tq),
        in_specs=[pl.BlockSpec((1, tq, c), lambda i, j: (i, j, 0)),
                  pl.BlockSpec((1, l, c), lambda i, j: (i, 0, 0)),
                  pl.BlockSpec((1, H_C, DV_AUG, l), lambda i, j: (i, 0, 0, 0)),
                  _const_spec(lam_params.shape), _const_spec((DV, 1))],
        out_specs=pl.BlockSpec((1, H_C, DV, tq), lambda i, j: (i, 0, 0, j)),
        out_shape=jax.ShapeDtypeStruct((b, H_C, DV, l), BF16),
        compiler_params=_cparams("parallel", "arbitrary"),
        name="diff_attention",
    )(q, k, v_t, lam_params, subln_g.reshape(DV, 1))


def _merge_body(x_ref, yf_ref, yb_ref, bonus_ref, glo_ref, ynb_ref, ydf_ref, gate_ref,
                g2_ref, lnx_ref, ones_ref, pa_ref, pb_ref, pc_ref, wout_ref, o_ref):
    ones = ones_ref[...]
    y = yf_ref[0] + yb_ref[0]
    mean = _split_dot(y, ones) * (1.0 / N_A)
    yc = y - mean
    var = _split_dot(yc * yc, ones) * (1.0 / N_A)
    lnx = lnx_ref[...]
    ya = yc * lax.rsqrt(var + LNX_EPS) * lnx[0:1] + lnx[1:2] + bonus_ref[0]
    gg = _dot(jax.nn.sigmoid(glo_ref[0]).astype(BF16), g2_ref[...])
    ya = (ya * gg).astype(BF16)
    gate = gate_ref[0].astype(F32)
    m = (gate[:, 0:D_MODEL] * _dot(ya, pa_ref[...])
         + gate[:, D_MODEL:2 * D_MODEL] * _dot(ynb_ref[0], pb_ref[...])
         + gate[:, 2 * D_MODEL:] * _dot(ydf_ref[0], pc_ref[...]))
    o_ref[0] = x_ref[0] + _dot(m.astype(BF16), wout_ref[...])


def _merge(x, yf, yb, bonus, za, y_nb, y_df, gate, g2, lnx, ones, pa, pb, pc, wout):
    b, l, d = x.shape
    tm = min(256, l)

    def tile(c):
        return pl.BlockSpec((1, tm, c), lambda i, j: (i, j, 0))

    glo_block = (3 * C_A + R_W + R_A) // R_G
    return pl.pallas_call(
        _merge_body,
        grid=(b, l // tm),
        in_specs=[tile(d), tile(C_A), tile(C_A), tile(C_A),
                  pl.BlockSpec((1, tm, R_G), lambda i, j: (i, j, glo_block)),
                  tile(C_B), tile(C_C), tile(GATE_COLS),
                  _const_spec(g2.shape), _const_spec(lnx.shape), _const_spec(ones.shape),
                  _const_spec(pa.shape), _const_spec(pb.shape), _const_spec(pc.shape), _const_spec(wout.shape)],
        out_specs=tile(d),
        out_shape=jax.ShapeDtypeStruct((b, l, d), F32),
        compiler_params=_cparams("parallel", "parallel"),
        name="merge",
    )(x, yf, yb, bonus, za, y_nb, y_df, gate, g2, lnx, ones, pa, pb, pc, wout)


def _head_ones():
    idx = np.arange(C_A) // N_A
    return jnp.asarray(idx[:, None] == idx[None, :], dtype=BF16)


def _pack_rwkv(p, l):
    w2, a2 = p['rwkv_w2'][l], p['rwkv_a2'][l]
    zero = jnp.zeros((R_W, C_A), F32)
    top = jnp.concatenate([w2[0], zero, w2[1], zero], axis=1)
    bot = jnp.concatenate([zero, a2[0], zero, a2[1]], axis=1)
    w2a = jnp.concatenate([top, bot], axis=0).astype(BF16)
    rk = p['rwkv_r_k'][l].reshape(2, C_A)
    vec = jnp.concatenate([p['rwkv_w0'][l], p['rwkv_a0'][l], p['rwkv_k_a'][l], rk,
                           p['rwkv_k_k'][l][None], jnp.zeros((7, C_A), F32)], axis=0).astype(F32)
    lnx = jnp.stack([p['rwkv_lnx_g'][l], p['rwkv_lnx_b'][l]]).astype(F32)
    return w2a, vec, lnx


def _token_mixing(x, p, l, ones):
    b, seq, _ = x.shape
    za, qb, kb, vb, qc, kc, vc, gate = _inproj(x, p['ln_mix_g'][l], p['w_in'][l], p['rwkv_mu'][l])
    w2a, vec, lnx = _pack_rwkv(p, l)
    intra = _rwkv_intra(za, w2a, vec, ones)
    yf, yb = _rwkv_scan(*intra[:8])
    y_nb = _natten(qb, kb, vb, _natten_bias_table(p['na_rpb'][l]))
    lam_init = 0.8 - 0.6 * math.exp(-0.3 * l)
    vc_t = jnp.transpose(vc.reshape(b, seq, H_C, DV), (0, 2, 3, 1))
    y_df = _diff_attention(qc, kc, vc_t, p['diff_lam'][l].astype(F32), p['diff_subln_g'][l].astype(F32), lam_init)
    y_df = jnp.transpose(y_df, (0, 3, 1, 2)).reshape(b, seq, C_C)
    return _merge(x, yf, yb, intra[8], za, y_nb, y_df, gate, p['rwkv_g2'][l].astype(BF16), lnx, ones,
                  p['p_a'][l].astype(BF16), p['p_b'][l].astype(BF16), p['p_c'][l].astype(BF16),
                  p['w_out'][l].astype(BF16))


def _trunk(x, p):
    b, seq, d = x.shape
    depth = p['w_in'].shape[0]
    ones = _head_ones()

    def ffn(x, pre, l, final_g=None):
        y = _ffn(x.reshape(b * seq, d), p['ln_' + pre + '_g'][l], p[pre + '_w_gate'][l].astype(BF16),
                 p[pre + '_w_up'][l].astype(BF16), p[pre + '_w_down'][l].astype(BF16), final_g)
        return y.reshape(b, seq, d)

    for l in range(depth):
        x = ffn(x, 'ffn1', l)
        x = _token_mixing(x, p, l, ones)
        x = ffn(x, 'ffn2', l, p['final_g'] if l == depth - 1 else None)
    return x


def kernel(x_prompt, x_sample, ln_ffn1_g, ffn1_w_gate, ffn1_w_up, ffn1_w_down, ln_mix_g, w_in, rwkv_mu, rwkv_w0, rwkv_w2, rwkv_a0, rwkv_a2, rwkv_k_a, rwkv_r_k, rwkv_k_k, rwkv_g2, rwkv_lnx_g, rwkv_lnx_b, na_rpb, diff_lam, diff_subln_g, p_a, p_b, p_c, w_out, ln_ffn2_g, ffn2_w_gate, ffn2_w_up, ffn2_w_down, final_g):
    p = dict(ln_ffn1_g=ln_ffn1_g, ffn1_w_gate=ffn1_w_gate, ffn1_w_up=ffn1_w_up, ffn1_w_down=ffn1_w_down,
             ln_mix_g=ln_mix_g, w_in=w_in, rwkv_mu=rwkv_mu, rwkv_w0=rwkv_w0, rwkv_w2=rwkv_w2,
             rwkv_a0=rwkv_a0, rwkv_a2=rwkv_a2, rwkv_k_a=rwkv_k_a, rwkv_r_k=rwkv_r_k, rwkv_k_k=rwkv_k_k,
             rwkv_g2=rwkv_g2, rwkv_lnx_g=rwkv_lnx_g, rwkv_lnx_b=rwkv_lnx_b, na_rpb=na_rpb,
             diff_lam=diff_lam, diff_subln_g=diff_subln_g, p_a=p_a, p_b=p_b, p_c=p_c, w_out=w_out,
             ln_ffn2_g=ln_ffn2_g, ffn2_w_gate=ffn2_w_gate, ffn2_w_up=ffn2_w_up, ffn2_w_down=ffn2_w_down,
             final_g=final_g)
    return (_trunk(x_prompt, p), _trunk(x_sample, p))
```
